```python
import math
import jax, jax.numpy as jnp
from jax import lax
import numpy as np

D_MODEL = 1024
BATCH = 32
SEQ = 256
DEPTH = 2
DEC_BATCH = 8
DEC_SEQ = 2048
PAST_LEN = 512

GRID_W = 64
N_MIXERS = 2
N_ATTN_LAYERS = (DEPTH + N_MIXERS - 1) // N_MIXERS
N_RET_LAYERS = DEPTH // N_MIXERS
ATTN_HEADS = 16
ATTN_KV_HEADS = 4
ATTN_HEAD_DIM = D_MODEL // ATTN_HEADS
ATTN_GROUPS = ATTN_HEADS // ATTN_KV_HEADS
Q_BLOCK = 128
ROPE_THETA = 10000.0
RET_HEADS = 4
RET_KEY_DIM = D_MODEL // RET_HEADS
RET_VALUE_DIM = 2 * D_MODEL // RET_HEADS
RET_CHUNK = 128
N_EXPERTS = 16
EXPERT_FF = 2 * D_MODEL
CAPACITY_FACTOR = 2
ALPHA = float((2 * DEPTH) ** 0.25)
BETA = float((8 * DEPTH) ** -0.25)

kernel_name = "hybrid_diffusion_attn_retention_ecmoe_step"


def layer_norm(x, g, b, eps=1e-5):
    xf = x.astype(jnp.float32)
    mu = jnp.mean(xf, axis=-1, keepdims=True)
    var = jnp.mean(jnp.square(xf - mu), axis=-1, keepdims=True)
    return ((xf - mu) * lax.rsqrt(var + eps)).astype(x.dtype) * g + b


def rms_norm(x, g, eps=1e-6):
    xf = x.astype(jnp.float32)
    return (xf * lax.rsqrt(jnp.mean(jnp.square(xf), axis=-1, keepdims=True) + eps)).astype(x.dtype) * g


def head_norm(x, eps=1e-6):
    xf = x.astype(jnp.float32)
    mu = jnp.mean(xf, axis=-1, keepdims=True)
    var = jnp.mean(jnp.square(xf - mu), axis=-1, keepdims=True)
    return ((xf - mu) * lax.rsqrt(var + eps)).astype(x.dtype)


def modulation(cvec, w, b):
    m = (jax.nn.silu(cvec) @ w + b)[:, None, :]
    return jnp.split(m, 6, axis=-1)


def modulate(x, shift, scale):
    return x * (1.0 + scale) + shift


def grid_positions(length):
    rows = length // GRID_W
    row = jnp.broadcast_to(jnp.arange(rows)[:, None], (rows, GRID_W)).reshape(-1)
    col = jnp.broadcast_to(jnp.arange(GRID_W)[None, :], (rows, GRID_W)).reshape(-1)
    return row, col


def rope_1d(x, pos):
    half = x.shape[-1] // 2
    freqs = ROPE_THETA ** (-jnp.arange(half, dtype=jnp.float32) / half)
    ang = pos.astype(jnp.float32)[:, None] * freqs[None, :]
    cos = jnp.cos(ang)[None, :, None, :].astype(x.dtype)
    sin = jnp.sin(ang)[None, :, None, :].astype(x.dtype)
    x1, x2 = x[..., :half], x[..., half:]
    return jnp.concatenate([x1 * cos - x2 * sin, x2 * cos + x1 * sin], axis=-1)


def axial_rope(x):
    row, col = grid_positions(x.shape[1])
    d = x.shape[-1] // 2
    return jnp.concatenate([rope_1d(x[..., :d], row), rope_1d(x[..., d:], col)], axis=-1)


def attn_project(h, w_qkv, q_gain, k_gain):
    B, L, _ = h.shape
    qkv = h @ w_qkv
    q, k, v = jnp.split(qkv, [ATTN_HEADS * ATTN_HEAD_DIM,
                              (ATTN_HEADS + ATTN_KV_HEADS) * ATTN_HEAD_DIM], axis=-1)
    q = rms_norm(q.reshape(B, L, ATTN_HEADS, ATTN_HEAD_DIM), q_gain)
    k = rms_norm(k.reshape(B, L, ATTN_KV_HEADS, ATTN_HEAD_DIM), k_gain)
    v = v.reshape(B, L, ATTN_KV_HEADS, ATTN_HEAD_DIM)
    return q, k, v


def blocked_attention(q, k, v):
    B, Lq = q.shape[:2]
    nb = Lq // Q_BLOCK
    qb = q.reshape(B, nb, Q_BLOCK, ATTN_KV_HEADS, ATTN_GROUPS, ATTN_HEAD_DIM)
    qb = jnp.moveaxis(qb, 1, 0)
    scale = ATTN_HEAD_DIM ** -0.5

    def one_block(qblk):
        s = jnp.einsum('bqhgd,bkhd->bhgqk', qblk, k).astype(jnp.float32) * scale
        p = jax.nn.softmax(s, axis=-1).astype(v.dtype)
        return jnp.einsum('bhgqk,bkhd->bqhgd', p, v)

    ob = lax.map(one_block, qb)
    return jnp.moveaxis(ob, 0, 1).reshape(B, Lq, ATTN_HEADS * ATTN_HEAD_DIM)


def attention_context(h, w_qkv, q_gain, k_gain, w_o):
    q, k, v = attn_project(h, w_qkv, q_gain, k_gain)
    return blocked_attention(q, k, v) @ w_o, k, v


def attention_latent(h, ctx_k, ctx_v, w_qkv, q_gain, k_gain, w_o):
    q, k, v = attn_project(h, w_qkv, q_gain, k_gain)
    q, k = axial_rope(q), axial_rope(k)
    k_all = jnp.concatenate([ctx_k.astype(k.dtype), k], axis=1)
    v_all = jnp.concatenate([ctx_v.astype(v.dtype), v], axis=1)
    return blocked_attention(q, k_all, v_all) @ w_o


def retention_scan(q, k, v, log_gamma, s0):
    B, L, H, _ = q.shape
    C = RET_CHUNK
    nc = L // C
    dt = q.dtype

    def chunks(t):
        return jnp.transpose(t.reshape(B, nc, C, H, t.shape[-1]), (1, 0, 3, 2, 4))

    pos = jnp.arange(C, dtype=jnp.float32)
    lg = log_gamma.astype(jnp.float32)[:, None]
    rel = pos[:, None] - pos[None, :]
    intra = jnp.where(rel >= 0, jnp.exp(lg[:, :, None] * jnp.maximum(rel, 0.0)), 0.0).astype(dt)
    q_dec = jnp.exp(lg * (pos + 1.0)).astype(dt)
    k_dec = jnp.exp(lg * (C - 1.0 - pos)).astype(dt)
    chunk_dec = jnp.exp(lg[:, 0] * C).astype(dt)

    def step(S, xs):
        qc, kc, vc = xs
        s = jnp.einsum('bhid,bhjd->bhij', qc, kc) * intra
        o = (jnp.einsum('bhij,bhjv->bhiv', s, vc)
             + jnp.einsum('bhid,bhdv->bhiv', qc * q_dec[:, :, None], S))
        S = S * chunk_dec[:, None, None] + jnp.einsum('bhjd,bhjv->bhdv', kc * k_dec[:, :, None], vc)
        return S, o

    S, o = lax.scan(step, s0.astype(dt), (chunks(q), chunks(k), chunks(v)))
    o = jnp.transpose(o, (1, 0, 3, 2, 4)).reshape(B, L, H, v.shape[-1])
    return o, S


def bidir_retention(q, k, v, decay_logit, s0_f, s0_b):
    lg = jax.nn.log_sigmoid(decay_logit.astype(jnp.float32))
    o_f, s_f = retention_scan(q, k, v, lg[0], s0_f)
    o_b, s_b = retention_scan(jnp.flip(q, 1), jnp.flip(k, 1), jnp.flip(v, 1), lg[1], s0_b)
    return o_f + jnp.flip(o_b, 1), s_f, s_b


def ret_project(h, w_in):
    B, L, _ = h.shape
    qk = RET_HEADS * RET_KEY_DIM
    vd = RET_HEADS * RET_VALUE_DIM
    q, k, v, g = jnp.split(h @ w_in, [qk, 2 * qk, 2 * qk + vd], axis=-1)
    q = q.reshape(B, L, RET_HEADS, RET_KEY_DIM)
    k = k.reshape(B, L, RET_HEADS, RET_KEY_DIM) * (RET_KEY_DIM ** -0.5)
    v = v.reshape(B, L, RET_HEADS, RET_VALUE_DIM)
    return q, k, v, g


def ret_output(o, g, w_o):
    B, L = o.shape[:2]
    return (jax.nn.silu(g) * head_norm(o).reshape(B, L, RET_HEADS * RET_VALUE_DIM)) @ w_o


def retention_context(h, w_in, decay_logit, w_o):
    q, k, v, g = ret_project(h, w_in)
    B = h.shape[0]
    s0 = jnp.zeros((B, RET_HEADS, RET_KEY_DIM, RET_VALUE_DIM), q.dtype)
    o, s_f, s_b = bidir_retention(q, k, v, decay_logit, s0, s0)
    return ret_output(o, g, w_o), jnp.stack([s_f, s_b], axis=1)


def retention_latent(h, ctx_state, w_in, decay_logit, w_o):
    q, k, v, g = ret_project(h, w_in)
    q, k = axial_rope(q), axial_rope(k)
    o, _, _ = bidir_retention(q, k, v, decay_logit, ctx_state[:, 0], ctx_state[:, 1])
    return ret_output(o, g, w_o)


def ec_moe(x, w_router, w_gate, w_up, w_down):
    B, L, D = x.shape
    n_tok = B * L
    xt = x.reshape(n_tok, D)
    aff = jax.nn.softmax((xt @ w_router).astype(jnp.float32), axis=-1)
    cap = CAPACITY_FACTOR * n_tok // N_EXPERTS
    gates, idx = lax.top_k(aff.T, cap)
    xe = xt[idx]
    hid = jax.nn.silu(jnp.einsum('ecd,edf->ecf', xe, w_gate)) * jnp.einsum('ecd,edf->ecf', xe, w_up)
    ye = jnp.einsum('ecf,efd->ecd', hid, w_down) * gates[:, :, None].astype(x.dtype)
    out = jnp.zeros_like(xt).at[idx.reshape(-1)].add(ye.reshape(-1, D))
    return out.reshape(B, L, D)


def setup_inputs(seed: int = 0) -> dict:
    key = jax.random.key(seed)
    ks = jax.random.split(key, 24)
    D = D_MODEL
    f32 = jnp.float32

    def nrm(k, shape, scale):
        return jax.random.normal(k, shape, f32) * scale

    x_prompt = nrm(ks[0], (BATCH, SEQ, D), 1.0)
    x_sample = nrm(ks[1], (DEC_BATCH, DEC_SEQ, D), 1.0)
    cache_attn_k = nrm(ks[2], (DEC_BATCH, N_ATTN_LAYERS, PAST_LEN, ATTN_KV_HEADS, ATTN_HEAD_DIM), 1.0)
    cache_attn_v = nrm(ks[3], (DEC_BATCH, N_ATTN_LAYERS, PAST_LEN, ATTN_KV_HEADS, ATTN_HEAD_DIM), 0.5)
    state_ret = nrm(ks[4], (DEC_BATCH, N_RET_LAYERS, 2, RET_HEADS, RET_KEY_DIM, RET_VALUE_DIM), 0.5)
    c = nrm(ks[5], (DEC_BATCH, D), 1.0)
    c_ctx = nrm(ks[6], (D,), 1.0)
    w_mod = nrm(ks[7], (DEPTH, D, 6 * D), 0.5 * D ** -0.5)
    b_mod = nrm(ks[8], (DEPTH, 6 * D), 0.01)
    ln_g = 1.0 + nrm(ks[9], (DEPTH, 2, D), 0.02)
    ln_b = nrm(ks[10], (DEPTH, 2, D), 0.02)
    attn_w_qkv = jnp.concatenate([
        nrm(ks[11], (N_ATTN_LAYERS, D, (ATTN_HEADS + ATTN_KV_HEADS) * ATTN_HEAD_DIM), D ** -0.5),
        nrm(ks[12], (N_ATTN_LAYERS, D, ATTN_KV_HEADS * ATTN_HEAD_DIM), BETA * D ** -0.5)], axis=-1)
    attn_q_gain = 1.0 + nrm(ks[13], (N_ATTN_LAYERS, ATTN_HEAD_DIM), 0.02)
    attn_k_gain = 1.0 + nrm(ks[14], (N_ATTN_LAYERS, ATTN_HEAD_DIM), 0.02)
    attn_w_o = nrm(ks[15], (N_ATTN_LAYERS, ATTN_HEADS * ATTN_HEAD_DIM, D),
                   BETA * (ATTN_HEADS * ATTN_HEAD_DIM) ** -0.5)
    ret_w_in = jnp.concatenate([
        nrm(ks[16], (N_RET_LAYERS, D, 2 * RET_HEADS * RET_KEY_DIM), D ** -0.5),
        nrm(ks[17], (N_RET_LAYERS, D, RET_HEADS * RET_VALUE_DIM), BETA * D ** -0.5),
        nrm(ks[18], (N_RET_LAYERS, D, RET_HEADS * RET_VALUE_DIM), D ** -0.5)], axis=-1)
    base_logit = jnp.asarray(np.log(2.0 ** (5 + np.arange(RET_HEADS)) - 1.0).astype(np.float32))
    ret_decay_logit = base_logit[None, None, :] + nrm(ks[19], (N_RET_LAYERS, 2, RET_HEADS), 0.1)
    ret_w_o = nrm(ks[20], (N_RET_LAYERS, RET_HEADS * RET_VALUE_DIM, D),
                  BETA * (RET_HEADS * RET_VALUE_DIM) ** -0.5)
    moe_w_router = nrm(ks[21], (DEPTH, D, N_EXPERTS), D ** -0.5)
    kg, ku = jax.random.split(ks[22])
    moe_w_gate = nrm(kg, (DEPTH, N_EXPERTS, D, EXPERT_FF), D ** -0.5)
    moe_w_up = nrm(ku, (DEPTH, N_EXPERTS, D, EXPERT_FF), D ** -0.5)
    moe_w_down = nrm(ks[23], (DEPTH, N_EXPERTS, EXPERT_FF, D), BETA * EXPERT_FF ** -0.5)
    return {
        "x_prompt": x_prompt, "x_sample": x_sample,
        "cache_attn_k": cache_attn_k, "cache_attn_v": cache_attn_v, "state_ret": state_ret,
        "c": c, "c_ctx": c_ctx, "w_mod": w_mod, "b_mod": b_mod, "ln_g": ln_g, "ln_b": ln_b,
        "attn_w_qkv": attn_w_qkv, "attn_q_gain": attn_q_gain, "attn_k_gain": attn_k_gain,
        "attn_w_o": attn_w_o, "ret_w_in": ret_w_in, "ret_decay_logit": ret_decay_logit,
        "ret_w_o": ret_w_o, "moe_w_router": moe_w_router, "moe_w_gate": moe_w_gate,
        "moe_w_up": moe_w_up, "moe_w_down": moe_w_down,
    }


def reference(x_prompt, x_sample, cache_attn_k, cache_attn_v, state_ret, c, c_ctx, w_mod, b_mod,
              ln_g, ln_b, attn_w_qkv, attn_q_gain, attn_k_gain, attn_w_o, ret_w_in,
              ret_decay_logit, ret_w_o, moe_w_router, moe_w_gate, moe_w_up, moe_w_down):
    xp, xs = x_prompt, x_sample
    new_k, new_v, new_s = [], [], []
    for i in range(DEPTH):
        j = i // N_MIXERS
        sh_p, sc_p, gt_p, sh2_p, sc2_p, gt2_p = modulation(c_ctx[None, :], w_mod[i], b_mod[i])
        sh_s, sc_s, gt_s, sh2_s, sc2_s, gt2_s = modulation(c, w_mod[i], b_mod[i])
        hp = modulate(xp, sh_p, sc_p)
        hs = modulate(xs, sh_s, sc_s)
        if i % N_MIXERS == 0:
            op, kp, vp = attention_context(hp, attn_w_qkv[j], attn_q_gain[j], attn_k_gain[j], attn_w_o[j])
            os_ = attention_latent(hs, cache_attn_k[:, j], cache_attn_v[:, j], attn_w_qkv[j],
                                   attn_q_gain[j], attn_k_gain[j], attn_w_o[j])
            new_k.append(kp)
            new_v.append(vp)
        else:
            op, sp = retention_context(hp, ret_w_in[j], ret_decay_logit[j], ret_w_o[j])
            os_ = retention_latent(hs, state_ret[:, j], ret_w_in[j], ret_decay_logit[j], ret_w_o[j])
            new_s.append(sp)
        xp = layer_norm(ALPHA * xp + gt_p * op, ln_g[i, 0], ln_b[i, 0])
        xs = layer_norm(ALPHA * xs + gt_s * os_, ln_g[i, 0], ln_b[i, 0])
        fp = ec_moe(modulate(xp, sh2_p, sc2_p), moe_w_router[i], moe_w_gate[i], moe_w_up[i], moe_w_down[i])
        fs = ec_moe(modulate(xs, sh2_s, sc2_s), moe_w_router[i], moe_w_gate[i], moe_w_up[i], moe_w_down[i])
        xp = layer_norm(ALPHA * xp + gt2_p * fp, ln_g[i, 1], ln_b[i, 1])
        xs = layer_norm(ALPHA * xs + gt2_s * fs, ln_g[i, 1], ln_b[i, 1])
    new_attn_k = jnp.stack(new_k, axis=1)
    new_attn_v = jnp.stack(new_v, axis=1)
    new_ret_state = jnp.stack(new_s, axis=1)
    return (xp, xs, new_attn_k, new_attn_v, new_ret_state)
```

```python
import functools
import math

import jax
import jax.numpy as jnp
from jax import lax
from jax.experimental import pallas as pl
from jax.experimental.pallas import tpu as pltpu

F32 = jnp.float32
BF16 = jnp.bfloat16

D_MODEL = 1024
BATCH, SEQ = 32, 256
DEC_BATCH, DEC_SEQ, PAST_LEN = 8, 2048, 512
DEPTH = 2
GRID_W = 64
N_MIXERS = 2
ATTN_HEADS, ATTN_KV_HEADS, ATTN_HEAD_DIM = 16, 4, 64
ATTN_GROUPS = ATTN_HEADS // ATTN_KV_HEADS
ROPE_THETA = 10000.0
RET_HEADS, RET_KEY_DIM, RET_VALUE_DIM, RET_CHUNK = 4, 256, 512, 128
N_EXPERTS, EXPERT_FF, CAPACITY_FACTOR = 16, 2048, 2
ALPHA = float((2 * DEPTH) ** 0.25)

N_PROMPT = BATCH * SEQ
N_SAMPLE = DEC_BATCH * DEC_SEQ
N_TOK = N_PROMPT + N_SAMPLE
N_MOD = 16
CAP_PROMPT = CAPACITY_FACTOR * N_PROMPT // N_EXPERTS
CAP_SAMPLE = CAPACITY_FACTOR * N_SAMPLE // N_EXPERTS
CAP_TOTAL = CAP_PROMPT + CAP_SAMPLE

VMEM_LIMIT_V7X = 56 * 1024 * 1024


def _params(sem):
    return pltpu.CompilerParams(dimension_semantics=sem, vmem_limit_bytes=VMEM_LIMIT_V7X)


def _mod_index(row0):
    return jnp.maximum(row0 - (N_PROMPT - DEC_SEQ), 0) // DEC_SEQ


def _layer_norm(y, g, b):
    mu = jnp.mean(y, axis=-1, keepdims=True)
    yc = y - mu
    var = jnp.mean(yc * yc, axis=-1, keepdims=True)
    return yc * lax.rsqrt(var + 1e-5) * g + b


def _dot(a, b):
    return jnp.dot(a, b, preferred_element_type=F32)


def _dot_nt(a, b):
    return lax.dot_general(a, b, (((1,), (1,)), ((), ())), preferred_element_type=F32)


def _dot_tn(a, b):
    return lax.dot_general(a, b, (((0,), (0,)), ((), ())), preferred_element_type=F32)


def _mod_kernel(c_ref, w_ref, b_ref, o_ref):
    c = c_ref[...]
    s = (c * jax.nn.sigmoid(c)).astype(BF16)
    o_ref[...] = _dot(s, w_ref[...].astype(BF16)) + b_ref[...]


def _modulation(cvec, w_mod, b_mod):
    tn = 1536
    n_out = 6 * D_MODEL
    return pl.pallas_call(
        _mod_kernel,
        out_shape=jax.ShapeDtypeStruct((DEPTH, N_MOD, n_out), F32),
        grid=(DEPTH, n_out // tn),
        in_specs=[
            pl.BlockSpec((N_MOD, D_MODEL), lambda l, j: (0, 0)),
            pl.BlockSpec((None, D_MODEL, tn), lambda l, j: (l, 0, j)),
            pl.BlockSpec((None, 1, tn), lambda l, j: (l, 0, j)),
        ],
        out_specs=pl.BlockSpec((None, N_MOD, tn), lambda l, j: (l, 0, j)),
        compiler_params=_params(("parallel", "parallel")),
        name="modulation",
    )(cvec, w_mod, b_mod.reshape(DEPTH, 1, n_out))


def _proj_kernel(x_ref, mod_ref, w_ref, o_ref, h_ref):
    @pl.when(pl.program_id(1) == 0)
    def _():
        shift = mod_ref[0:1, :]
        scale = mod_ref[1:2, :]
        h_ref[...] = (x_ref[...] * (1.0 + scale) + shift).astype(BF16)

    o_ref[...] = _dot(h_ref[...], w_ref[...].astype(BF16)).astype(o_ref.dtype)


def _project(x, mods, w, tm=1024, tn=512):
    n_out = w.shape[1]
    return pl.pallas_call(
        _proj_kernel,
        out_shape=jax.ShapeDtypeStruct((N_TOK, n_out), F32),
        grid=(N_TOK // tm, n_out // tn),
        in_specs=[
            pl.BlockSpec((tm, D_MODEL), lambda i, j: (i, 0)),
            pl.BlockSpec((None, 6, D_MODEL), lambda i, j: (_mod_index(i * tm), 0, 0)),
            pl.BlockSpec((D_MODEL, tn), lambda i, j: (0, j)),
        ],
        out_specs=pl.BlockSpec((tm, tn), lambda i, j: (i, j)),
        scratch_shapes=[pltpu.VMEM((tm, D_MODEL), BF16)],
        compiler_params=_params(("parallel", "arbitrary")),
        name="project",
    )(x, mods, w)


QK_TILE = 256


def _qknorm_kernel(qkv_ref, cos_ref, sin_ref, gq_ref, gk_ref, bd_ref, qn_ref, kn_ref, kb_ref, vb_ref):
    bd = bd_ref[...]
    cos = cos_ref[...]
    sin = sin_ref[...]
    cos2 = jnp.concatenate([cos, cos], axis=1)
    sin2 = jnp.concatenate([sin, sin], axis=1)
    lane = lax.broadcasted_iota(jnp.int32, cos2.shape, 1)
    first = (lane % 32) < 16

    def normed_rope(x, gain):
        sq = x * x
        hi = sq.astype(BF16)
        lo = (sq - hi.astype(F32)).astype(BF16)
        ms = (_dot(hi, bd) + _dot(lo, bd)) * (1.0 / ATTN_HEAD_DIM)
        xn = x * lax.rsqrt(ms + 1e-6) * gain
        partner = jnp.where(first, pltpu.roll(xn, 256 - 16, 1), pltpu.roll(xn, 16, 1))
        return xn * cos2 + partner * sin2

    for cb in range(4):
        cols = slice(cb * 256, (cb + 1) * 256)
        qn = normed_rope(qkv_ref[:, cols], gq_ref[:, cols])
        qn_ref[:, cols] = (qn * (ATTN_HEAD_DIM ** -0.5)).astype(BF16)
    kn = normed_rope(qkv_ref[:, 1024:1280], gk_ref[...])
    kn_ref[...] = kn
    kb_ref[...] = kn.astype(BF16)
    vb_ref[...] = qkv_ref[:, 1280:1536].astype(BF16)


def _attn_rope_tables():
    half = ATTN_HEAD_DIM // 4
    t = jnp.arange(DEC_SEQ)
    row = (t // GRID_W).astype(F32)
    col = (t % GRID_W).astype(F32)
    freqs = ROPE_THETA ** (-jnp.arange(half, dtype=F32) / half)
    ang_r = row[:, None] * freqs[None, :]
    ang_c = col[:, None] * freqs[None, :]

    def pair(ang):
        return (jnp.concatenate([jnp.cos(ang), jnp.cos(ang)], axis=1),
                jnp.concatenate([-jnp.sin(ang), jnp.sin(ang)], axis=1))

    cr, sr = pair(ang_r)
    cc, sc = pair(ang_c)
    cos64 = jnp.concatenate([cr, cc], axis=1)
    sin64 = jnp.concatenate([sr, sc], axis=1)
    cos = jnp.concatenate([cos64, cos64], axis=1)
    sin = jnp.concatenate([sin64, sin64], axis=1)
    cos = jnp.concatenate([cos, jnp.ones((QK_TILE, 128), F32)], axis=0)
    sin = jnp.concatenate([sin, jnp.zeros((QK_TILE, 128), F32)], axis=0)
    return cos, sin


def _qk_norm_rope(qkv, q_gain, k_gain):
    tm = QK_TILE
    cos, sin = _attn_rope_tables()
    gq = jnp.tile(q_gain, ATTN_HEADS).reshape(1, ATTN_HEADS * ATTN_HEAD_DIM)
    gk = jnp.tile(k_gain, ATTN_KV_HEADS).reshape(1, ATTN_KV_HEADS * ATTN_HEAD_DIM)
    seg = jnp.arange(256) // ATTN_HEAD_DIM
    bd = (seg[:, None] == seg[None, :]).astype(BF16)
    n_prompt_tiles = N_PROMPT // tm
    tiles_per_seq = DEC_SEQ // tm

    def table_index(i):
        return (jnp.where(i < n_prompt_tiles, tiles_per_seq, (i - n_prompt_tiles) % tiles_per_seq), 0)

    kvw = ATTN_KV_HEADS * ATTN_HEAD_DIM
    return pl.pallas_call(
        _qknorm_kernel,
        out_shape=(
            jax.ShapeDtypeStruct((N_TOK, D_MODEL), BF16),
            jax.ShapeDtypeStruct((N_TOK, kvw), F32),
            jax.ShapeDtypeStruct((N_TOK, kvw), BF16),
            jax.ShapeDtypeStruct((N_TOK, kvw), BF16),
        ),
        grid=(N_TOK // tm,),
        in_specs=[
            pl.BlockSpec((tm, qkv.shape[1]), lambda i: (i, 0)),
            pl.BlockSpec((tm, 128), table_index),
            pl.BlockSpec((tm, 128), table_index),
            pl.BlockSpec((1, D_MODEL), lambda i: (0, 0)),
            pl.BlockSpec((1, kvw), lambda i: (0, 0)),
            pl.BlockSpec((256, 256), lambda i: (0, 0)),
        ],
        out_specs=(
            pl.BlockSpec((tm, D_MODEL), lambda i: (i, 0)),
            pl.BlockSpec((tm, kvw), lambda i: (i, 0)),
            pl.BlockSpec((tm, kvw), lambda i: (i, 0)),
            pl.BlockSpec((tm, kvw), lambda i: (i, 0)),
        ),
        compiler_params=_params(("parallel",)),
        name="qk_norm_rope",
    )(qkv, cos, sin, gq, gk, bd)


def _attn_kernel(q_ref, k_ref, v_ref, o_ref):
    tq = q_ref.shape[0]
    hd = ATTN_HEAD_DIM
    for hk in range(ATTN_KV_HEADS):
        kh = k_ref[:, hk * hd:(hk + 1) * hd]
        vh = v_ref[:, hk * hd:(hk + 1) * hd]
        heads = [hk * ATTN_GROUPS + g for g in range(ATTN_GROUPS)]
        qg = jnp.concatenate([q_ref[:, h * hd:(h + 1) * hd] for h in heads], axis=0)
        s = _dot_nt(qg, kh)
        m = jnp.max(s, axis=-1, keepdims=True)
        p = jnp.exp(s - m)
        l = jnp.sum(p, axis=-1, keepdims=True)
        o = _dot(p.astype(BF16), vh) / l
        for g, h in enumerate(heads):
            o_ref[:, h * hd:(h + 1) * hd] = o[g * tq:(g + 1) * tq].astype(o_ref.dtype)


def _attention(q, k, v, n_seq, lq, lk, q_row0, tq):
    q_tiles = lq // tq
    q_block0 = q_row0 // tq
    kvw = ATTN_KV_HEADS * ATTN_HEAD_DIM
    return pl.pallas_call(
        _attn_kernel,
        out_shape=jax.ShapeDtypeStruct((n_seq * lq, D_MODEL), BF16),
        grid=(n_seq, q_tiles),
        in_specs=[
            pl.BlockSpec((tq, D_MODEL), lambda b, i: (q_block0 + b * q_tiles + i, 0)),
            pl.BlockSpec((None, lk, kvw), lambda b, i: (b, 0, 0)),
            pl.BlockSpec((None, lk, kvw), lambda b, i: (b, 0, 0)),
        ],
        out_specs=pl.BlockSpec((tq, D_MODEL), lambda b, i: (b * q_tiles + i, 0)),
        compiler_params=_params(("parallel", "parallel")),
        name="attention",
    )(q, k, v)


def _oproj_kernel(a_ref, w_ref, x_ref, mod_ref, g_ref, b_ref, wr_ref, x1_ref, xt_ref, aff_ref, wb_ref):
    @pl.when(pl.program_id(0) == 0)
    def _():
        wb_ref[...] = w_ref[...].astype(BF16)

    o = _dot(a_ref[...], wb_ref[...])
    gate = mod_ref[2:3, :]
    x1 = _layer_norm(ALPHA * x_ref[...] + gate * o, g_ref[...], b_ref[...])
    x1_ref[...] = x1
    xt = x1 * (1.0 + mod_ref[4:5, :]) + mod_ref[3:4, :]
    xh = xt.astype(BF16)
    xt_ref[...] = xh
    xl = (xt - xh.astype(F32)).astype(BF16)
    wr = wr_ref[...]
    wh = wr.astype(BF16)
    wl = (wr - wh.astype(F32)).astype(BF16)
    logits = _dot_nt(wh, xh) + _dot_nt(wl, xh) + _dot_nt(wh, xl)
    e = jnp.exp(logits - jnp.max(logits, axis=0, keepdims=True))
    aff_ref[...] = e / jnp.sum(e, axis=0, keepdims=True)


def _out_project(a, w_o, x, mods, ln_g, ln_b, w_router, tm=512):
    k = a.shape[1]
    row = lambda i: (i, 0)
    const = lambda i: (0, 0)
    return pl.pallas_call(
        _oproj_kernel,
        out_shape=(
            jax.ShapeDtypeStruct((N_TOK, D_MODEL), F32),
            jax.ShapeDtypeStruct((N_TOK, D_MODEL), BF16),
            jax.ShapeDtypeStruct((N_EXPERTS, N_TOK), F32),
        ),
        grid=(N_TOK // tm,),
        in_specs=[
            pl.BlockSpec((tm, k), row),
            pl.BlockSpec((k, D_MODEL), const),
            pl.BlockSpec((tm, D_MODEL), row),
            pl.BlockSpec((None, 6, D_MODEL), lambda i: (_mod_index(i * tm), 0, 0)),
            pl.BlockSpec((1, D_MODEL), const),
            pl.BlockSpec((1, D_MODEL), const),
            pl.BlockSpec((N_EXPERTS, D_MODEL), const),
        ],
        out_specs=(
            pl.BlockSpec((tm, D_MODEL), row),
            pl.BlockSpec((tm, D_MODEL), row),
            pl.BlockSpec((N_EXPERTS, tm), lambda i: (0, i)),
        ),
        scratch_shapes=[pltpu.VMEM((k, D_MODEL), BF16)],
        compiler_params=_params(("arbitrary",)),
        name="out_project",
    )(a, w_o, x, mods, ln_g.reshape(1, D_MODEL), ln_b.reshape(1, D_MODEL), w_router.T)


def _ffn_kernel(x_ref, wg_ref, wu_ref, wd_ref, gate_ref, o_ref):
    f = pl.program_id(2)
    x = x_ref[...]
    hg = _dot(x, wg_ref[...].astype(BF16))
    hu = _dot(x, wu_ref[...].astype(BF16))
    hid = (hg * jax.nn.sigmoid(hg) * hu).astype(BF16)
    c = _dot(hid, wd_ref[...].astype(BF16))

    @pl.when(f == 0)
    def _():
        o_ref[...] = c

    @pl.when(f > 0)
    def _():
        o_ref[...] += c

    @pl.when(f == pl.num_programs(2) - 1)
    def _():
        o_ref[...] *= gate_ref[...]


def _expert_ffn(xe, gates, w_gate, w_up, w_down, tr=1024, tf=512):
    return pl.pallas_call(
        _ffn_kernel,
        out_shape=jax.ShapeDtypeStruct((N_EXPERTS, CAP_TOTAL, D_MODEL), F32),
        grid=(N_EXPERTS, CAP_TOTAL // tr, EXPERT_FF // tf),
        in_specs=[
            pl.BlockSpec((None, tr, D_MODEL), lambda e, r, f: (e, r, 0)),
            pl.BlockSpec((None, D_MODEL, tf), lambda e, r, f: (e, 0, f)),
            pl.BlockSpec((None, D_MODEL, tf), lambda e, r, f: (e, 0, f)),
            pl.BlockSpec((None, tf, D_MODEL), lambda e, r, f: (e, f, 0)),
            pl.BlockSpec((None, tr, 1), lambda e, r, f: (e, r, 0)),
        ],
        out_specs=pl.BlockSpec((None, tr, D_MODEL), lambda e, r, f: (e, r, 0)),
        compiler_params=_params(("parallel", "parallel", "arbitrary")),
        name="expert_ffn",
    )(xe, w_gate, w_up, w_down, gates)


def _combine_kernel(x_ref, f_ref, mod_ref, g_ref, b_ref, o_ref):
    gate = mod_ref[5:6, :]
    o_ref[...] = _layer_norm(ALPHA * x_ref[...] + gate * f_ref[...], g_ref[...], b_ref[...])


def _combine(x1, f, mods, ln_g, ln_b, tm=512):
    row = lambda i: (i, 0)
    const = lambda i: (0, 0)
    return pl.pallas_call(
        _combine_kernel,
        out_shape=jax.ShapeDtypeStruct((N_TOK, D_MODEL), F32),
        grid=(N_TOK // tm,),
        in_specs=[
            pl.BlockSpec((tm, D_MODEL), row),
            pl.BlockSpec((tm, D_MODEL), row),
            pl.BlockSpec((None, 6, D_MODEL), lambda i: (_mod_index(i * tm), 0, 0)),
            pl.BlockSpec((1, D_MODEL), const),
            pl.BlockSpec((1, D_MODEL), const),
        ],
        out_specs=pl.BlockSpec((tm, D_MODEL), row),
        compiler_params=_params(("parallel",)),
        name="moe_combine",
    )(x1, f, mods, ln_g.reshape(1, D_MODEL), ln_b.reshape(1, D_MODEL))


def _ret_kernel(lg_ref, *refs, rope, has_s0, emit_state):
    refs = list(refs)
    q_ref, k_ref, v_ref, g_ref = refs[:4]
    refs = refs[4:]
    if rope:
        cosr_ref, sinr_ref, cosc_ref, sinc_ref = refs[:4]
        refs = refs[4:]
    if has_s0:
        s0_ref = refs[0]
        refs = refs[1:]
    o_ref = refs[0]
    refs = refs[1:]
    if emit_state:
        sf_ref = refs[0]
        refs = refs[1:]
    qs_ref, ks_ref, s_ref, acc_ref = refs

    h = pl.program_id(1)
    seq_len = q_ref.shape[0]
    c = RET_CHUNK
    n_chunks = seq_len // c
    dk = RET_KEY_DIM

    q = q_ref[...]
    k = k_ref[...] * (dk ** -0.5)
    if rope:
        def rot(x):
            x_row, x_col = x[:, :128], x[:, 128:]
            y_row = x_row * cosr_ref[...] + pltpu.roll(x_row, 64, 1) * sinr_ref[...]
            y_col = x_col * cosc_ref[...] + pltpu.roll(x_col, 64, 1) * sinc_ref[...]
            return jnp.concatenate([y_row, y_col], axis=1)
        q = rot(q)
        k = rot(k)
    qs_ref[...] = q
    ks_ref[...] = k

    pos_k = lax.broadcasted_iota(jnp.int32, (c, dk), 0).astype(F32)
    ri = lax.broadcasted_iota(jnp.int32, (c, c), 0)
    ci = lax.broadcasted_iota(jnp.int32, (c, c), 1)

    for direction in range(2):
        lg = lg_ref[direction, h]
        if direction == 0:
            rel = (ri - ci).astype(F32)
            q_dec = jnp.exp(lg * (pos_k + 1.0))
            k_dec = jnp.exp(lg * (c - 1.0 - pos_k))
        else:
            rel = (ci - ri).astype(F32)
            q_dec = jnp.exp(lg * (c - pos_k))
            k_dec = jnp.exp(lg * pos_k)
        intra = jnp.where(rel >= 0, jnp.exp(lg * jnp.maximum(rel, 0.0)), 0.0)
        chunk_dec = jnp.exp(jnp.full((1, RET_VALUE_DIM), lg * c, F32))

        if has_s0:
            s_ref[...] = s0_ref[direction]
        else:
            s_ref[...] = jnp.zeros_like(s_ref)

        def step(i, carry, direction=direction, intra=intra, q_dec=q_dec, k_dec=k_dec, chunk_dec=chunk_dec):
            ch = i if direction == 0 else n_chunks - 1 - i
            rows = pl.ds(pl.multiple_of(ch * c, c), c)
            qc = qs_ref[rows, :]
            kc = ks_ref[rows, :]
            vc = v_ref[rows, :].astype(BF16)
            state = s_ref[...]
            s = _dot_nt(qc.astype(BF16), kc.astype(BF16)) * intra
            o = _dot(s.astype(BF16), vc) + _dot((qc * q_dec).astype(BF16), state.astype(BF16))
            if direction == 0:
                acc_ref[rows, :] = o
            else:
                acc_ref[rows, :] += o
            s_ref[...] = state * chunk_dec + _dot_tn((kc * k_dec).astype(BF16), vc)
            return carry

        lax.fori_loop(0, n_chunks, step, 0)
        if emit_state:
            sf_ref[direction] = s_ref[...]

    o = acc_ref[...]
    mu = jnp.mean(o, axis=-1, keepdims=True)
    oc = o - mu
    var = jnp.mean(oc * oc, axis=-1, keepdims=True)
    hn = oc * lax.rsqrt(var + 1e-6)
    g = g_ref[...]
    o_ref[...] = (g * jax.nn.sigmoid(g) * hn).astype(o_ref.dtype)


def _ret_rope_tables():
    half = RET_KEY_DIM // 4
    t = jnp.arange(DEC_SEQ)
    freqs = ROPE_THETA ** (-jnp.arange(half, dtype=F32) / half)

    def pair(pos):
        ang = pos.astype(F32)[:, None] * freqs[None, :]
        return (jnp.concatenate([jnp.cos(ang), jnp.cos(ang)], axis=1),
                jnp.concatenate([-jnp.sin(ang), jnp.sin(ang)], axis=1))

    cr, sr = pair(t // GRID_W)
    cc, sc = pair(t % GRID_W)
    return cr, sr, cc, sc


def _retention(proj, log_gamma, n_seq, seq_len, row0, s0=None, emit_state=False, rope=False):
    dk, dv, nh = RET_KEY_DIM, RET_VALUE_DIM, RET_HEADS
    rb0 = row0 // seq_len
    k_col0 = nh * dk // dk
    v_col0 = 2 * nh * dk // dv
    g_col0 = (2 * nh * dk + nh * dv) // dv
    in_specs = [
        pl.BlockSpec((seq_len, dk), lambda b, h, lg: (rb0 + b, h)),
        pl.BlockSpec((seq_len, dk), lambda b, h, lg: (rb0 + b, k_col0 + h)),
        pl.BlockSpec((seq_len, dv), lambda b, h, lg: (rb0 + b, v_col0 + h)),
        pl.BlockSpec((seq_len, dv), lambda b, h, lg: (rb0 + b, g_col0 + h)),
    ]
    args = [proj, proj, proj, proj]
    if rope:
        in_specs += [pl.BlockSpec((seq_len, 128), lambda b, h, lg: (0, 0))] * 4
        args += list(_ret_rope_tables())
    state_spec = pl.BlockSpec((None, 2, None, dk, dv), lambda b, h, lg: (b, 0, h, 0, 0))
    if s0 is not None:
        in_specs.append(state_spec)
        args.append(s0)
    out_shape = [jax.ShapeDtypeStruct((n_seq * seq_len, nh * dv), BF16)]
    out_specs = [pl.BlockSpec((seq_len, dv), lambda b, h, lg: (b, h))]
    if emit_state:
        out_shape.append(jax.ShapeDtypeStruct((n_seq, 2, nh, dk, dv), F32))
        out_specs.append(state_spec)
    kern = functools.partial(_ret_kernel, rope=rope, has_s0=s0 is not None, emit_state=emit_state)
    return pl.pallas_call(
        kern,
        out_shape=tuple(out_shape),
        grid_spec=pltpu.PrefetchScalarGridSpec(
            num_scalar_prefetch=1,
            grid=(n_seq, nh),
            in_specs=in_specs,
            out_specs=tuple(out_specs),
            scratch_shapes=[
                pltpu.VMEM((seq_len, dk), F32),
                pltpu.VMEM((seq_len, dk), F32),
                pltpu.VMEM((dk, dv), F32),
                pltpu.VMEM((seq_len, dv), F32),
            ],
        ),
        compiler_params=_params(("parallel", "parallel")),
        name="retention",
    )(log_gamma, *args)


def _route_group(aff, cap):
    gates, idx = lax.top_k(aff, cap)
    return gates, idx


def _moe(xt, aff, w_gate, w_up, w_down):
    g_p, i_p = _route_group(aff[:, :N_PROMPT], CAP_PROMPT)
    g_s, i_s = _route_group(aff[:, N_PROMPT:], CAP_SAMPLE)
    idx = jnp.concatenate([i_p, i_s + N_PROMPT], axis=1)
    gates = jnp.concatenate([g_p, g_s], axis=1)
    xe = xt[idx]
    ye = _expert_ffn(xe, gates[:, :, None], w_gate, w_up, w_down)
    out = jnp.zeros((N_TOK, D_MODEL), F32).at[idx.reshape(-1)].add(ye.reshape(-1, D_MODEL))
    return out


def kernel(x_prompt, x_sample, cache_attn_k, cache_attn_v, state_ret, c, c_ctx, w_mod, b_mod, ln_g, ln_b,
           attn_w_qkv, attn_q_gain, attn_k_gain, attn_w_o, ret_w_in, ret_decay_logit, ret_w_o,
           moe_w_router, moe_w_gate, moe_w_up, moe_w_down):
    x = jnp.concatenate([x_prompt.reshape(N_PROMPT, D_MODEL), x_sample.reshape(N_SAMPLE, D_MODEL)], axis=0)
    cvec = jnp.concatenate([c_ctx[None, :], c, jnp.zeros((N_MOD - 1 - DEC_BATCH, D_MODEL), F32)], axis=0)
    mods_all = _modulation(cvec, w_mod, b_mod).reshape(DEPTH, N_MOD, 6, D_MODEL)
    kvw = ATTN_KV_HEADS * ATTN_HEAD_DIM
    new_k = new_v = new_s = None

    for i in range(DEPTH):
        j = i // N_MIXERS
        mods = mods_all[i]
        if i % N_MIXERS == 0:
            qkv = _project(x, mods, attn_w_qkv[j])
            qn, kn, kb, vb = _qk_norm_rope(qkv, attn_q_gain[j], attn_k_gain[j])
            new_k = kn[:N_PROMPT].reshape(BATCH, SEQ, ATTN_KV_HEADS, ATTN_HEAD_DIM)
            new_v = qkv[:N_PROMPT, D_MODEL + kvw:].reshape(BATCH, SEQ, ATTN_KV_HEADS, ATTN_HEAD_DIM)
            a_p = _attention(qn, kb[:N_PROMPT].reshape(BATCH, SEQ, kvw), vb[:N_PROMPT].reshape(BATCH, SEQ, kvw),
                             BATCH, SEQ, SEQ, 0, SEQ)
            k_all = jnp.concatenate([cache_attn_k[:, j].reshape(DEC_BATCH, PAST_LEN, kvw).astype(BF16),
                                     kb[N_PROMPT:].reshape(DEC_BATCH, DEC_SEQ, kvw)], axis=1)
            v_all = jnp.concatenate([cache_attn_v[:, j].reshape(DEC_BATCH, PAST_LEN, kvw).astype(BF16),
                                     vb[N_PROMPT:].reshape(DEC_BATCH, DEC_SEQ, kvw)], axis=1)
            a_s = _attention(qn, k_all, v_all, DEC_BATCH, DEC_SEQ, PAST_LEN + DEC_SEQ, N_PROMPT, 128)
            a = jnp.concatenate([a_p, a_s], axis=0)
            w_o = attn_w_o[j]
        else:
            proj = _project(x, mods, ret_w_in[j])
            lg = jax.nn.log_sigmoid(ret_decay_logit[j].astype(F32))
            a_p, new_s = _retention(proj, lg, BATCH, SEQ, 0, emit_state=True)
            (a_s,) = _retention(proj, lg, DEC_BATCH, DEC_SEQ, N_PROMPT, s0=state_ret[:, j], rope=True)
            a = jnp.concatenate([a_p, a_s], axis=0)
            w_o = ret_w_o[j]
        x1, xt, aff = _out_project(a, w_o, x, mods, ln_g[i, 0], ln_b[i, 0], moe_w_router[i])
        f = _moe(xt, aff, moe_w_gate[i], moe_w_up[i], moe_w_down[i])
        x = _combine(x1, f, mods, ln_g[i, 1], ln_b[i, 1])

    y_prompt = x[:N_PROMPT].reshape(BATCH, SEQ, D_MODEL)
    y_sample = x[N_PROMPT:].reshape(DEC_BATCH, DEC_SEQ, D_MODEL)
    return (y_prompt, y_sample, new_k[:, None], new_v[:, None], new_s[:, None])
```
